```python
import functools
import jax, jax.numpy as jnp
from jax import lax
import numpy as np

D_MODEL = 1024
BATCH = 8
SEQ = 2048
DEPTH = 2
DEC_BATCH = 32
DEC_SEQ = 4
PAST_LEN = 16384
PAGE_SIZE = 128

POOL_W = D_MODEL // 4
POOL_WINDOWS = (2, 4, 8, 16)
POOL_GROUP = POOL_W // len(POOL_WINDOWS)
POOL_CTX = max(POOL_WINDOWS) - 1
N_HEADS = 8
HEAD_DIM = 64
ATT_W = N_HEADS * HEAD_DIM
MOBA_BLOCK = 256
MOBA_TOPK = 3
Q_CHUNK = 128
ROPE_THETA = 10000.0
CONV_W = D_MODEL // 4
CONV_K = 3
N_BRANCH = 3
IN_W = POOL_W + 3 * ATT_W + 3 * CONV_W + N_BRANCH * D_MODEL
N_EXPERTS = 32
TOP_K = 4
D_FF = D_MODEL
SWIGLU_LIMIT = 7.0
SWIGLU_ALPHA = 1.702
MOE_BLOCK = 128
RMS_EPS = 1e-6
NEG_INF = -1e30

kernel_name = 'hybrid_pool_moba_conv_moe_decode_step'


def rms_norm(x, g):
    xf = x.astype(jnp.float32)
    y = xf * lax.rsqrt(jnp.mean(xf * xf, axis=-1, keepdims=True) + RMS_EPS)
    return (y * g.astype(jnp.float32)).astype(x.dtype)


def rope(x, pos):
    half = HEAD_DIM // 2
    inv = ROPE_THETA ** (-jnp.arange(half, dtype=jnp.float32) / half)
    ang = pos.astype(jnp.float32)[:, None] * inv[None, :]
    cos = jnp.cos(ang)[None, :, None, :]
    sin = jnp.sin(ang)[None, :, None, :]
    xf = x.astype(jnp.float32)
    x1, x2 = xf[..., :half], xf[..., half:]
    return jnp.concatenate([x1 * cos - x2 * sin, x2 * cos + x1 * sin], axis=-1).astype(x.dtype)


def pool_mix(u, ctx, pos, pool_w, pool_scale):
    b, t, _ = u.shape
    z = jnp.concatenate([ctx.astype(u.dtype), u], axis=1)
    zf = z.astype(jnp.float32)
    cs = jnp.concatenate([jnp.zeros((b, 1, POOL_W), jnp.float32), jnp.cumsum(zf, axis=1)], axis=1)
    end = cs[:, POOL_CTX + 1:]
    means = []
    for g, w in enumerate(POOL_WINDOWS):
        sl = slice(g * POOL_GROUP, (g + 1) * POOL_GROUP)
        start = cs[:, POOL_CTX + 1 - w:POOL_CTX + 1 - w + t, sl]
        cnt = jnp.minimum(pos + 1, w).astype(jnp.float32)[None, :, None]
        means.append((end[..., sl] - start) / cnt)
    m = jnp.concatenate(means, axis=-1) - u.astype(jnp.float32)
    m = m.reshape(b, t, len(POOL_WINDOWS), POOL_GROUP)
    y = jnp.einsum('btgc,gce->btge', m, pool_w.astype(jnp.float32)).reshape(b, t, POOL_W)
    y = y * pool_scale.astype(jnp.float32)
    return y.astype(u.dtype), z[:, -POOL_CTX:]


def short_conv(u, ctx, conv_w):
    t = u.shape[1]
    z = jnp.concatenate([ctx.astype(u.dtype), u], axis=1)
    y = conv_w[0] * z[:, 0:t]
    for i in range(1, CONV_K):
        y = y + conv_w[i] * z[:, i:i + t]
    return y, z[:, -(CONV_K - 1):]


def block_attend(q, k_sel, v_sel, sel_valid, k_own, v_own, own_mask):
    qf = q.astype(jnp.float32) * (HEAD_DIM ** -0.5)
    lo = jnp.einsum('bthd,blhd->bhtl', qf, k_own.astype(jnp.float32))
    lo = jnp.where(own_mask, lo, NEG_INF)
    if k_sel is None:
        p = jax.nn.softmax(lo, axis=-1)
        out = jnp.einsum('bhtl,blhd->bthd', p, v_own.astype(jnp.float32))
        return out.astype(q.dtype)
    ls = jnp.einsum('bthd,bhtmd->bhtm', qf, k_sel.astype(jnp.float32))
    if sel_valid is not None:
        ls = jnp.where(sel_valid, ls, NEG_INF)
    m = ls.shape[-1]
    p = jax.nn.softmax(jnp.concatenate([ls, lo], axis=-1), axis=-1)
    out = (jnp.einsum('bhtm,bhtmd->bthd', p[..., :m], v_sel.astype(jnp.float32))
           + jnp.einsum('bhtl,blhd->bthd', p[..., m:], v_own.astype(jnp.float32)))
    return out.astype(q.dtype)


def prompt_attend(q, k, v):
    b, s = q.shape[0], q.shape[1]
    nb = -(-s // MOBA_BLOCK)
    pad = ((0, 0), (0, nb * MOBA_BLOCK - s), (0, 0), (0, 0))
    kp, vp = jnp.pad(k, pad), jnp.pad(v, pad)
    kb = kp.reshape(b, nb, MOBA_BLOCK, N_HEADS, HEAD_DIM).transpose(0, 3, 1, 2, 4)
    vb = vp.reshape(b, nb, MOBA_BLOCK, N_HEADS, HEAD_DIM).transpose(0, 3, 1, 2, 4)
    kmean = jnp.mean(kb.astype(jnp.float32), axis=3)
    k_sel = min(MOBA_TOPK, nb)
    bidx = jnp.arange(b)[:, None, None, None]
    hidx = jnp.arange(N_HEADS)[None, :, None, None]

    def chunk(ci):
        q0 = ci * Q_CHUNK
        qc = lax.dynamic_slice_in_dim(q, q0, Q_CHUNK, axis=1)
        own = q0 // MOBA_BLOCK
        k_own = lax.dynamic_slice_in_dim(kp, own * MOBA_BLOCK, MOBA_BLOCK, axis=1)
        v_own = lax.dynamic_slice_in_dim(vp, own * MOBA_BLOCK, MOBA_BLOCK, axis=1)
        own_mask = (own * MOBA_BLOCK + jnp.arange(MOBA_BLOCK))[None, :] <= (q0 + jnp.arange(Q_CHUNK))[:, None]
        scores = jnp.einsum('bthd,bhnd->bhtn', qc.astype(jnp.float32), kmean)
        scores = jnp.where(jnp.arange(nb) < own, scores, NEG_INF)
        _, idx = lax.top_k(scores, k_sel)
        ks = kb[bidx, hidx, idx].reshape(b, N_HEADS, Q_CHUNK, k_sel * MOBA_BLOCK, HEAD_DIM)
        vs = vb[bidx, hidx, idx].reshape(b, N_HEADS, Q_CHUNK, k_sel * MOBA_BLOCK, HEAD_DIM)
        valid = jnp.repeat(jnp.arange(k_sel) < own, MOBA_BLOCK)
        return block_attend(qc, ks, vs, valid, k_own, v_own, own_mask)

    out = lax.map(chunk, jnp.arange(s // Q_CHUNK))
    return out.transpose(1, 0, 2, 3, 4).reshape(b, s, N_HEADS, HEAD_DIM)


def sample_attend(q, k, v, ck, cv, page_table):
    db, t = q.shape[0], q.shape[1]
    n_pages = page_table.shape[1]
    ppb = MOBA_BLOCK // PAGE_SIZE
    n_fb = n_pages // ppb
    own_pages = page_table[:, n_fb * ppb:]
    n_rem = (n_pages - n_fb * ppb) * PAGE_SIZE
    k_own = jnp.concatenate([ck[own_pages].reshape(db, n_rem, N_HEADS, HEAD_DIM).astype(k.dtype), k], axis=1)
    v_own = jnp.concatenate([cv[own_pages].reshape(db, n_rem, N_HEADS, HEAD_DIM).astype(v.dtype), v], axis=1)
    own_mask = jnp.concatenate([jnp.ones((t, n_rem), bool), jnp.tril(jnp.ones((t, t), bool))], axis=1)
    k_sel = min(MOBA_TOPK, n_fb)
    if k_sel == 0:
        return block_attend(q, None, None, None, k_own, v_own, own_mask)
    kpast = ck[page_table[:, :n_fb * ppb]].reshape(db, n_fb, MOBA_BLOCK, N_HEADS, HEAD_DIM)
    kmean = jnp.mean(kpast.astype(jnp.float32), axis=2)
    scores = jnp.einsum('bthd,bnhd->bhtn', q.astype(jnp.float32), kmean)
    _, idx = lax.top_k(scores, k_sel)
    cols = idx[..., None] * ppb + jnp.arange(ppb)
    phys = page_table[jnp.arange(db)[:, None, None, None, None], cols]
    hidx = jnp.arange(N_HEADS)[None, :, None, None, None, None]
    rows = jnp.arange(PAGE_SIZE)
    ks = ck[phys[..., None], rows, hidx].reshape(db, N_HEADS, t, k_sel * MOBA_BLOCK, HEAD_DIM)
    vs = cv[phys[..., None], rows, hidx].reshape(db, N_HEADS, t, k_sel * MOBA_BLOCK, HEAD_DIM)
    return block_attend(q, ks.astype(q.dtype), vs.astype(v.dtype), None, k_own, v_own, own_mask)


def moe_ffn(x, w_router, b_router, w_up, b_up, w_down, b_down):
    n, d = x.shape
    nk = n * TOP_K
    logits = x.astype(jnp.float32) @ w_router.astype(jnp.float32) + b_router.astype(jnp.float32)
    top_v, top_e = lax.top_k(logits, TOP_K)
    gate = jax.nn.softmax(top_v, axis=-1).reshape(nk)
    e_flat = top_e.reshape(nk)
    tok_flat = jnp.repeat(jnp.arange(n, dtype=jnp.int32), TOP_K)
    order = jnp.argsort(e_flat)
    e_sorted = e_flat[order]
    counts = jnp.bincount(e_flat, length=N_EXPERTS)
    padded = (counts + MOE_BLOCK - 1) // MOE_BLOCK * MOE_BLOCK
    start = jnp.cumsum(counts) - counts
    pad_end = jnp.cumsum(padded)
    pad_start = pad_end - padded
    dest = pad_start[e_sorted] + jnp.arange(nk) - start[e_sorted]
    n_blk = -(-(nk + N_EXPERTS * (MOE_BLOCK - 1)) // MOE_BLOCK)
    n_rows = n_blk * MOE_BLOCK
    row_tok = jnp.full((n_rows,), n, jnp.int32).at[dest].set(tok_flat[order])
    row_gate = jnp.zeros((n_rows,), jnp.float32).at[dest].set(gate[order])
    blk_e = jnp.minimum(jnp.searchsorted(pad_end, jnp.arange(n_blk) * MOE_BLOCK, side='right'), N_EXPERTS - 1)
    xb = jnp.concatenate([x, jnp.zeros((1, d), x.dtype)], axis=0)[row_tok].reshape(n_blk, MOE_BLOCK, d)

    def expert_block(args):
        xe, e = args
        hu = xe @ w_up[e] + b_up[e]
        a, lin = jnp.split(hu, 2, axis=-1)
        a = jnp.minimum(a, SWIGLU_LIMIT)
        lin = jnp.clip(lin, -SWIGLU_LIMIT, SWIGLU_LIMIT)
        return (a * jax.nn.sigmoid(SWIGLU_ALPHA * a) * (lin + 1)) @ w_down[e] + b_down[e]

    yb = lax.map(expert_block, (xb, blk_e)).reshape(n_rows, d)
    y = jnp.zeros((n + 1, d), jnp.float32).at[row_tok].add(yb.astype(jnp.float32) * row_gate[:, None])
    return y[:n].astype(x.dtype)


def hybrid_layer(x, c, pos, pool_ctx, conv_ctx, attend, p):
    b, t, _ = x.shape
    mod = jax.nn.silu(c) @ p['w_ada'] + p['b_ada']
    sh1, sc1, gt1, sh2, sc2, gt2 = jnp.split(mod[:, None, :], 6, axis=-1)
    h = rms_norm(x, p['g_mix']) * (1 + sc1) + sh1
    proj = h @ p['w_in']
    cuts = np.cumsum([POOL_W, ATT_W, ATT_W, ATT_W, CONV_W, CONV_W, CONV_W]).tolist()
    u_pool, q, k, v, g_b, g_c, u_conv, gate_logits = jnp.split(proj, cuts, axis=-1)
    q = rope(q.reshape(b, t, N_HEADS, HEAD_DIM), pos)
    k = rope(k.reshape(b, t, N_HEADS, HEAD_DIM), pos)
    v = v.reshape(b, t, N_HEADS, HEAD_DIM)
    y_att = attend(q, k, v).reshape(b, t, ATT_W) @ p['w_att_o']
    y_pool, pool_state = pool_mix(u_pool, pool_ctx, pos, p['pool_w'], p['pool_scale'])
    y_pool = y_pool @ p['w_pool_up']
    y_c, conv_state = short_conv(g_c * u_conv, conv_ctx, p['conv_w'])
    y_conv = (g_b * y_c) @ p['w_conv_out']
    gates = jax.nn.sigmoid(gate_logits.reshape(b, t, N_BRANCH, D_MODEL))
    merged = gates[:, :, 0] * y_pool + gates[:, :, 1] * y_att + gates[:, :, 2] * y_conv
    x = x + gt1 * (merged @ p['w_out'])
    h2 = rms_norm(x, p['g_ffn']) * (1 + sc2) + sh2
    ffn = moe_ffn(h2.reshape(b * t, D_MODEL), p['w_router'], p['b_router'], p['w_up'], p['b_up'],
                  p['w_down'], p['b_down']).reshape(b, t, D_MODEL)
    x = x + gt2 * ffn
    return x, k, v, pool_state, conv_state


def setup_inputs(seed: int = 0) -> dict:
    key = jax.random.key(seed)
    keys = iter(jax.random.split(key, 32))

    def nrm(shape, scale=1.0):
        return jax.random.normal(next(keys), shape, jnp.float32) * scale

    D = D_MODEL
    n_pages = PAST_LEN // PAGE_SIZE
    n_used = DEC_BATCH * n_pages
    n_phys = n_used + max(1, n_used // 4)
    page_table = jax.random.permutation(next(keys), n_phys)[:n_used].reshape(DEC_BATCH, n_pages).astype(jnp.int32)
    return {
        'x_prompt': nrm((BATCH, SEQ, D)),
        'x_sample': nrm((DEC_BATCH, DEC_SEQ, D)),
        'cache_k': nrm((DEPTH, n_phys, PAGE_SIZE, N_HEADS, HEAD_DIM)),
        'cache_v': nrm((DEPTH, n_phys, PAGE_SIZE, N_HEADS, HEAD_DIM)),
        'page_table': page_table,
        'state_pool': nrm((DEPTH, DEC_BATCH, POOL_CTX, POOL_W)),
        'state_conv': nrm((DEPTH, DEC_BATCH, CONV_K - 1, CONV_W)),
        'c_prompt': nrm((BATCH, D)),
        'c_sample': nrm((DEC_BATCH, D)),
        'w_ada': nrm((DEPTH, D, 6 * D), 0.5 * D ** -0.5),
        'b_ada': nrm((DEPTH, 6 * D), 0.02),
        'g_mix': 1.0 + nrm((DEPTH, D), 0.05),
        'g_ffn': 1.0 + nrm((DEPTH, D), 0.05),
        'w_in': nrm((DEPTH, D, IN_W), D ** -0.5),
        'pool_w': nrm((DEPTH, len(POOL_WINDOWS), POOL_GROUP, POOL_GROUP), POOL_GROUP ** -0.5),
        'pool_scale': 1.0 + nrm((DEPTH, POOL_W), 0.1),
        'w_pool_up': nrm((DEPTH, POOL_W, D), POOL_W ** -0.5),
        'w_att_o': nrm((DEPTH, ATT_W, D), ATT_W ** -0.5),
        'conv_w': nrm((DEPTH, CONV_K, CONV_W), CONV_K ** -0.5),
        'w_conv_out': nrm((DEPTH, CONV_W, D), CONV_W ** -0.5),
        'w_out': nrm((DEPTH, D, D), D ** -0.5),
        'w_router': nrm((DEPTH, D, N_EXPERTS), D ** -0.5),
        'b_router': nrm((DEPTH, N_EXPERTS), 0.01),
        'w_up': nrm((DEPTH, N_EXPERTS, D, 2 * D_FF), D ** -0.5),
        'b_up': nrm((DEPTH, N_EXPERTS, 2 * D_FF), 0.01),
        'w_down': nrm((DEPTH, N_EXPERTS, D_FF, D), D_FF ** -0.5),
        'b_down': nrm((DEPTH, N_EXPERTS, D), 0.01),
        'g_final': 1.0 + nrm((D,), 0.05),
    }


def reference(x_prompt, x_sample, cache_k, cache_v, page_table, state_pool, state_conv, c_prompt, c_sample,
              w_ada, b_ada, g_mix, g_ffn, w_in, pool_w, pool_scale, w_pool_up, w_att_o, conv_w, w_conv_out,
              w_out, w_router, b_router, w_up, b_up, w_down, b_down, g_final):
    b, s = x_prompt.shape[0], x_prompt.shape[1]
    past_len = page_table.shape[1] * PAGE_SIZE
    pos_p = jnp.arange(s)
    pos_s = past_len + jnp.arange(x_sample.shape[1])
    pool_ctx0 = jnp.zeros((b, POOL_CTX, POOL_W), x_prompt.dtype)
    conv_ctx0 = jnp.zeros((b, CONV_K - 1, CONV_W), x_prompt.dtype)
    hp, hs = x_prompt, x_sample
    kp_l, vp_l, ks_l, vs_l, pp_l, ps_l, cp_l, cs_l = [], [], [], [], [], [], [], []
    for l in range(DEPTH):
        p = {'w_ada': w_ada[l], 'b_ada': b_ada[l], 'g_mix': g_mix[l], 'g_ffn': g_ffn[l], 'w_in': w_in[l],
             'pool_w': pool_w[l], 'pool_scale': pool_scale[l], 'w_pool_up': w_pool_up[l], 'w_att_o': w_att_o[l],
             'conv_w': conv_w[l], 'w_conv_out': w_conv_out[l], 'w_out': w_out[l], 'w_router': w_router[l],
             'b_router': b_router[l], 'w_up': w_up[l], 'b_up': b_up[l], 'w_down': w_down[l], 'b_down': b_down[l]}
        hp, k_new, v_new, pool_new, conv_new = hybrid_layer(hp, c_prompt, pos_p, pool_ctx0, conv_ctx0,
                                                            prompt_attend, p)
        kp_l.append(k_new.reshape(b, s // PAGE_SIZE, PAGE_SIZE, N_HEADS, HEAD_DIM))
        vp_l.append(v_new.reshape(b, s // PAGE_SIZE, PAGE_SIZE, N_HEADS, HEAD_DIM))
        pp_l.append(pool_new)
        cp_l.append(conv_new)
        attend_s = functools.partial(sample_attend, ck=cache_k[l], cv=cache_v[l], page_table=page_table)
        hs, k_new, v_new, pool_new, conv_new = hybrid_layer(hs, c_sample, pos_s, state_pool[l], state_conv[l],
                                                            attend_s, p)
        ks_l.append(k_new)
        vs_l.append(v_new)
        ps_l.append(pool_new)
        cs_l.append(conv_new)
    y_prompt = rms_norm(hp, g_final)
    y_sample = rms_norm(hs, g_final)
    return (y_prompt, y_sample, jnp.stack(kp_l), jnp.stack(vp_l), jnp.stack(ks_l), jnp.stack(vs_l),
            jnp.stack(pp_l), jnp.stack(ps_l), jnp.stack(cp_l), jnp.stack(cs_l))
```

```python
import functools

import numpy as np
import jax
import jax.numpy as jnp
from jax import lax
from jax.experimental import pallas as pl
from jax.experimental.pallas import tpu as pltpu

F32 = jnp.float32
BF16 = jnp.bfloat16
I32 = jnp.int32

D_MODEL = 1024
DEPTH = 2
PAGE_SIZE = 128
POOL_W = 256
POOL_WINDOWS = (2, 4, 8, 16)
POOL_GROUP = 64
POOL_CTX = 15
N_HEADS = 8
HEAD_DIM = 64
ATT_W = 512
MOBA_BLOCK = 256
MOBA_TOPK = 3
ROPE_THETA = 10000.0
CONV_W = 256
CONV_K = 3
IN_W = POOL_W + 3 * ATT_W + 3 * CONV_W + 3 * D_MODEL
N_EXPERTS = 32
TOP_K = 4
D_FF = 1024
SWIGLU_LIMIT = 7.0
SWIGLU_ALPHA = 1.702
RMS_EPS = 1e-6
NEG_INF = -1e30

C_POOL, C_Q, C_K, C_V, C_GB, C_GC, C_UC, C_GATE = 0, 256, 768, 1280, 1792, 2048, 2304, 2560

V7X_VMEM_BYTES = 64 * 1024 * 1024
LANES = 128
SUBLANES = 8

TM_PROJ = 256
TM_MIX = 256
TM_COMB = 512
TM_MOE = 256
PAGES_PER_CHUNK = 8


def _cparams(sem, vmem_mb):
    return pltpu.CompilerParams(dimension_semantics=sem, vmem_limit_bytes=vmem_mb * 1024 * 1024)


def _nt(a, b):
    return lax.dot_general(a, b, (((1,), (1,)), ((), ())), preferred_element_type=F32)


def _split(a):
    hi = a.astype(BF16)
    lo = (a - hi.astype(F32)).astype(BF16)
    return hi, lo


def _nt3(a, b):
    ah, al = _split(a)
    bh, bl = _split(b)
    return _nt(ah, bh) + (_nt(ah, bl) + _nt(al, bh))


def _mm3(a, b):
    ah, al = _split(a)
    bh, bl = _split(b)
    d = functools.partial(jnp.dot, preferred_element_type=F32)
    return d(ah, bh) + (d(ah, bl) + d(al, bh))


def _rms(x, g):
    return x * lax.rsqrt(jnp.mean(x * x, axis=-1, keepdims=True) + RMS_EPS) * g


def _ada_kernel(c_ref, w_ref, b_ref, o_ref):
    c = c_ref[...]
    s = (c * jax.nn.sigmoid(c)).astype(BF16)
    o_ref[0] = jnp.dot(s, w_ref[0].astype(BF16), preferred_element_type=F32) + b_ref[0]


def _ada_mod(c_all, w_ada, b_ada):
    nb = c_all.shape[0]
    tn = 1536
    return pl.pallas_call(
        _ada_kernel,
        grid=(DEPTH, 6 * D_MODEL // tn),
        in_specs=[
            pl.BlockSpec((nb, D_MODEL), lambda l, j: (0, 0)),
            pl.BlockSpec((1, D_MODEL, tn), lambda l, j: (l, 0, j)),
            pl.BlockSpec((1, 1, tn), lambda l, j: (l, 0, j)),
        ],
        out_specs=pl.BlockSpec((1, nb, tn), lambda l, j: (l, 0, j)),
        out_shape=jax.ShapeDtypeStruct((DEPTH, nb, 6 * D_MODEL), F32),
        compiler_params=_cparams(("arbitrary", "arbitrary"), 32),
        name="ada_mod",
    )(c_all, w_ada, b_ada.reshape(DEPTH, 1, 6 * D_MODEL))


def _rope_table_kernel(inv_ref, cos_ref, sin_ref, *, offset):
    rows = cos_ref.shape[0]
    pos = (lax.broadcasted_iota(I32, (rows, LANES), 0) + offset).astype(F32)
    lane = lax.broadcasted_iota(I32, (rows, LANES), 1)
    ang = pos * inv_ref[...]
    sign = jnp.where((lane % HEAD_DIM) < HEAD_DIM // 2, -1.0, 1.0)
    cos_ref[...] = jnp.cos(ang)
    sin_ref[...] = jnp.sin(ang) * sign


def _rope_tables(inv_lanes, rows, offset):
    return pl.pallas_call(
        functools.partial(_rope_table_kernel, offset=offset),
        out_shape=(jax.ShapeDtypeStruct((rows, LANES), F32), jax.ShapeDtypeStruct((rows, LANES), F32)),
        name="rope_table",
    )(inv_lanes)


def _rope(x, cos, sin, first_half):
    outs = []
    for c in range(x.shape[1] // LANES):
        xc = x[:, c * LANES:(c + 1) * LANES]
        fwd = pltpu.roll(xc, LANES - HEAD_DIM // 2, 1)
        bwd = pltpu.roll(xc, HEAD_DIM // 2, 1)
        outs.append(xc * cos + jnp.where(first_half, fwd, bwd) * sin)
    return jnp.concatenate(outs, axis=1)


def _proj_kernel(x_ref, sh_ref, sc_ref, g_ref, cos_ref, sin_ref, w_ref,
                 up_ref, q_ref, k_ref, v_ref, gb_ref, zc_ref, gates_ref):
    x = x_ref[...]
    h = (_rms(x, g_ref[...]) * (1.0 + sc_ref[...]) + sh_ref[...]).astype(BF16)

    def mm(lo, hi):
        return jnp.dot(h, w_ref[:, lo:hi], preferred_element_type=F32)

    cos = cos_ref[...]
    sin = sin_ref[...]
    lane = lax.broadcasted_iota(I32, cos.shape, 1)
    first_half = (lane % HEAD_DIM) < HEAD_DIM // 2
    up_ref[...] = mm(C_POOL, C_Q)
    q_ref[...] = _rope(mm(C_Q, C_K), cos, sin, first_half) * (HEAD_DIM ** -0.5)
    k_ref[...] = _rope(mm(C_K, C_V), cos, sin, first_half)
    v_ref[...] = mm(C_V, C_GB)
    gb_ref[...] = mm(C_GB, C_GC)
    zc_ref[...] = mm(C_GC, C_UC) * mm(C_UC, C_GATE)
    for j in range(3):
        lo = C_GATE + j * D_MODEL
        gates_ref[:, j * D_MODEL:(j + 1) * D_MODEL] = jax.nn.sigmoid(mm(lo, lo + D_MODEL)).astype(BF16)


def _proj_out_shapes(lead):
    def s(w, dt=F32):
        return jax.ShapeDtypeStruct(lead + (w,), dt)
    return (s(POOL_W), s(ATT_W), s(ATT_W), s(ATT_W), s(CONV_W), s(CONV_W), s(3 * D_MODEL, BF16))


_PROJ_WIDTHS = (POOL_W, ATT_W, ATT_W, ATT_W, CONV_W, CONV_W, 3 * D_MODEL)


def _proj_prompt(x, mod4, layer_b0, g_mix, cos, sin, w_in):
    b, s, _ = x.shape
    tm = TM_PROJ
    row = lambda w: pl.BlockSpec((None, tm, w), lambda bi, i: (bi, i, 0))
    return pl.pallas_call(
        _proj_kernel,
        grid=(b, s // tm),
        in_specs=[
            row(D_MODEL),
            pl.BlockSpec((None, None, 1, D_MODEL), lambda bi, i: (bi + layer_b0, 0, 0, 0)),
            pl.BlockSpec((None, None, 1, D_MODEL), lambda bi, i: (bi + layer_b0, 1, 0, 0)),
            pl.BlockSpec((1, D_MODEL), lambda bi, i: (0, 0)),
            pl.BlockSpec((tm, LANES), lambda bi, i: (i, 0)),
            pl.BlockSpec((tm, LANES), lambda bi, i: (i, 0)),
            pl.BlockSpec((D_MODEL, IN_W), lambda bi, i: (0, 0)),
        ],
        out_specs=tuple(row(w) for w in _PROJ_WIDTHS),
        out_shape=_proj_out_shapes((b, s)),
        compiler_params=_cparams(("arbitrary", "arbitrary"), 56),
        name="proj_prompt",
    )(x, mod4, mod4, g_mix, cos, sin, w_in)


def _proj_sample(x, mod_rows, g_mix, cos, sin, w_in):
    n = x.shape[0]
    full = lambda w: pl.BlockSpec((n, w), lambda i: (0, 0))
    return pl.pallas_call(
        _proj_kernel,
        grid=(1,),
        in_specs=[
            full(D_MODEL),
            pl.BlockSpec((n, D_MODEL), lambda i: (0, 0)),
            pl.BlockSpec((n, D_MODEL), lambda i: (0, 1)),
            pl.BlockSpec((1, D_MODEL), lambda i: (0, 0)),
            full(LANES), full(LANES),
            pl.BlockSpec((D_MODEL, IN_W), lambda i: (0, 0)),
        ],
        out_specs=tuple(full(w) for w in _PROJ_WIDTHS),
        out_shape=_proj_out_shapes((n,)),
        compiler_params=_cparams(("arbitrary",), 56),
        name="proj_sample",
    )(x, mod_rows, mod_rows, g_mix, cos, sin, w_in)


def _attn_prompt_kernel(q_ref, k_ref, v_ref, o_ref, kb_ref, vt_ref, km_ref):
    qi = pl.program_id(2)
    nblk = km_ref.shape[0]
    blk = MOBA_BLOCK

    @pl.when(qi == 0)
    def _():
        k = k_ref[...]
        kb_ref[...] = k.astype(BF16)
        km_ref[...] = jnp.mean(k.reshape(nblk, blk, LANES), axis=1)
        for n in range(nblk):
            vt_ref[n] = v_ref[n * blk:(n + 1) * blk, :].T.astype(BF16)

    q = q_ref[...]
    lane = lax.broadcasted_iota(I32, (1, LANES), 1)
    own0 = pl.multiple_of(qi * blk, blk)
    k_own = kb_ref[pl.ds(own0, blk), :]
    key_i = lax.broadcasted_iota(I32, (blk, blk), 0)
    qry_i = lax.broadcasted_iota(I32, (blk, blk), 1)
    causal = key_i <= qry_i
    bidx = lax.broadcasted_iota(I32, (nblk, blk), 0)
    past = bidx < qi
    outs = []
    for hh in range(2):
        qh = jnp.where((lane // HEAD_DIM) == hh, q, 0.0)
        qh_b = qh.astype(BF16)
        sc = jnp.where(past, _nt3(km_ref[...], qh), NEG_INF)
        rank = jnp.zeros((nblk, blk), I32)
        for m in range(nblk):
            sm = sc[m:m + 1, :]
            rank = rank + ((sm > sc) | ((sm == sc) & (m < bidx))).astype(I32)
        sel = jnp.where(past & (rank < MOBA_TOPK), 1.0, 0.0)

        s = jnp.where(causal, _nt(k_own, qh_b), NEG_INF)
        m_run = jnp.max(s, axis=0, keepdims=True)
        p = jnp.exp(s - m_run)
        l_run = jnp.sum(p, axis=0, keepdims=True)
        acc = jnp.dot(vt_ref[qi, hh * HEAD_DIM:(hh + 1) * HEAD_DIM, :], p.astype(BF16),
                      preferred_element_type=F32)

        def body(n, carry):
            m_run, l_run, acc = carry
            kn = kb_ref[pl.ds(pl.multiple_of(n * blk, blk), blk), :]
            seln = jnp.sum(jnp.where(bidx == n, sel, 0.0), axis=0, keepdims=True)
            s = jnp.where(seln > 0.5, _nt(kn, qh_b), NEG_INF)
            m_new = jnp.maximum(m_run, jnp.max(s, axis=0, keepdims=True))
            alpha = jnp.exp(m_run - m_new)
            p = jnp.exp(s - m_new)
            l_new = alpha * l_run + jnp.sum(p, axis=0, keepdims=True)
            pv = jnp.dot(vt_ref[n, hh * HEAD_DIM:(hh + 1) * HEAD_DIM, :], p.astype(BF16),
                         preferred_element_type=F32)
            return m_new, l_new, alpha * acc + pv

        m_run, l_run, acc = lax.fori_loop(0, qi, body, (m_run, l_run, acc))
        outs.append(acc / l_run)
    o_ref[...] = jnp.concatenate(outs, axis=0).T.astype(BF16)


def _attn_prompt(q, k, v):
    b, s, _ = q.shape
    nblk = s // MOBA_BLOCK
    return pl.pallas_call(
        _attn_prompt_kernel,
        grid=(b, ATT_W // LANES, nblk),
        in_specs=[
            pl.BlockSpec((None, MOBA_BLOCK, LANES), lambda bi, hp, qi: (bi, qi, hp)),
            pl.BlockSpec((None, s, LANES), lambda bi, hp, qi: (bi, 0, hp)),
            pl.BlockSpec((None, s, LANES), lambda bi, hp, qi: (bi, 0, hp)),
        ],
        out_specs=pl.BlockSpec((None, MOBA_BLOCK, LANES), lambda bi, hp, qi: (bi, qi, hp)),
        out_shape=jax.ShapeDtypeStruct((b, s, ATT_W), BF16),
        scratch_shapes=[
            pltpu.VMEM((s, LANES), BF16),
            pltpu.VMEM((nblk, LANES, MOBA_BLOCK), BF16),
            pltpu.VMEM((nblk, LANES), F32),
        ],
        compiler_params=_cparams(("arbitrary", "arbitrary", "arbitrary"), 32),
        name="attn_prompt",
    )(q, k, v)


def _sample_rows(ref, b, n_tok, n_batch):
    rows = [ref[pl.ds(t * n_batch + b, 1), :] for t in range(n_tok)]
    rows.append(jnp.zeros((SUBLANES - n_tok, ref.shape[1]), ref.dtype))
    return jnp.concatenate(rows, axis=0)


def _sample_select_kernel(pt_ref, q_ref, ck_hbm, idx_ref, buf, kmt_ref, sem, *, layer, n_tok, n_pages):
    b = pl.program_id(0)
    nb = pl.num_programs(0)
    ch = PAGES_PER_CHUNK
    n_chunks = n_pages // ch
    ppb = MOBA_BLOCK // PAGE_SIZE
    n_blocks = n_pages // ppb
    lane = lax.broadcasted_iota(I32, (1, LANES), 1)

    def issue(bb, c, slot):
        for p in range(ch):
            phys = pt_ref[bb * n_pages + c * ch + p]
            pltpu.make_async_copy(ck_hbm.at[layer, phys], buf.at[slot, p], sem.at[slot]).start()

    @pl.when(b == 0)
    def _():
        issue(0, 0, 0)

    kmt_ref[...] = jnp.zeros_like(kmt_ref)

    def chunk_body(c, carry):
        slot = c % 2

        @pl.when(c + 1 < n_chunks)
        def _():
            issue(b, c + 1, 1 - slot)

        @pl.when((c + 1 == n_chunks) & (b + 1 < nb))
        def _():
            issue(b + 1, 0, 1 - slot)

        pltpu.make_async_copy(buf.at[slot], buf.at[slot], sem.at[slot]).wait()
        for jb in range(ch // ppb):
            tot = None
            for pg in range(ppb):
                page = buf[slot, jb * ppb + pg].reshape(N_HEADS * HEAD_DIM, PAGE_SIZE)
                tot = page if tot is None else tot + page
            rs = jnp.sum(tot, axis=-1, keepdims=True)
            n = c * (ch // ppb) + jb
            kmt_ref[...] += jnp.where(lane == n, rs, 0.0)
        return carry

    lax.fori_loop(0, n_chunks, chunk_body, 0)

    q8 = _sample_rows(q_ref, b, n_tok, nb)
    for h in range(N_HEADS):
        kmh = kmt_ref[h * HEAD_DIM:(h + 1) * HEAD_DIM, :] * (1.0 / MOBA_BLOCK)
        qh = q8[:, h * HEAD_DIM:(h + 1) * HEAD_DIM]
        sc = jnp.where(lane < n_blocks, _mm3(qh, kmh), NEG_INF)
        out = jnp.zeros((SUBLANES, LANES), I32)
        for j in range(MOBA_TOPK):
            mx = jnp.max(sc, axis=1, keepdims=True)
            ix = jnp.min(jnp.where(sc == mx, lane, LANES), axis=1, keepdims=True)
            out = jnp.where(lane == j, ix, out)
            sc = jnp.where(lane == ix, -jnp.inf, sc)
        idx_ref[0, h * SUBLANES:(h + 1) * SUBLANES, :] = out


def _sample_select(page_flat, q_rows, ck_t, layer, n_batch, n_tok, n_pages):
    kern = functools.partial(_sample_select_kernel, layer=layer, n_tok=n_tok, n_pages=n_pages)
    return pl.pallas_call(
        kern,
        grid_spec=pltpu.PrefetchScalarGridSpec(
            num_scalar_prefetch=1,
            grid=(n_batch,),
            in_specs=[
                pl.BlockSpec(q_rows.shape, lambda b, pt: (0, 0)),
                pl.BlockSpec(memory_space=pl.ANY),
            ],
            out_specs=pl.BlockSpec((1, N_HEADS * SUBLANES, LANES), lambda b, pt: (b, 0, 0)),
            scratch_shapes=[
                pltpu.VMEM((2, PAGES_PER_CHUNK, N_HEADS, HEAD_DIM, PAGE_SIZE), F32),
                pltpu.VMEM((N_HEADS * HEAD_DIM, LANES), F32),
                pltpu.SemaphoreType.DMA((2,)),
            ],
        ),
        out_shape=jax.ShapeDtypeStruct((n_batch, N_HEADS * SUBLANES, LANES), I32),
        compiler_params=_cparams(("arbitrary",), 32),
        name="sample_select",
    )(page_flat, q_rows, ck_t)


def _sample_attend_kernel(pt_ref, sel_ref, q_ref, kn_ref, vn_ref, ck_hbm, cv_hbm, o_ref,
                          kbuf, vbuf, sem, *, layer, n_tok, n_pages):
    b = pl.program_id(0)
    nb = pl.num_programs(0)
    slot = b % 2
    ppb = MOBA_BLOCK // PAGE_SIZE

    def issue(bb, sl):
        for h in range(N_HEADS):
            for t in range(n_tok):
                for j in range(MOBA_TOPK):
                    n = sel_ref[((bb * N_HEADS + h) * n_tok + t) * MOBA_TOPK + j]
                    for pg in range(ppb):
                        phys = pt_ref[bb * n_pages + n * ppb + pg]
                        col = (j * ppb + pg) * PAGE_SIZE
                        pltpu.make_async_copy(ck_hbm.at[layer, phys, h],
                                              kbuf.at[sl, h, t, :, pl.ds(col, PAGE_SIZE)], sem.at[sl]).start()
                        pltpu.make_async_copy(cv_hbm.at[layer, phys, h],
                                              vbuf.at[sl, h, t, :, pl.ds(col, PAGE_SIZE)], sem.at[sl]).start()

    @pl.when(b == 0)
    def _():
        issue(0, 0)

    @pl.when(b + 1 < nb)
    def _():
        issue(b + 1, 1 - slot)

    pltpu.make_async_copy(kbuf.at[slot], kbuf.at[slot], sem.at[slot]).wait()
    pltpu.make_async_copy(vbuf.at[slot], vbuf.at[slot], sem.at[slot]).wait()

    q8 = _sample_rows(q_ref, b, n_tok, nb)
    k8 = _sample_rows(kn_ref, b, n_tok, nb)
    v8 = _sample_rows(vn_ref, b, n_tok, nb)
    ri = lax.broadcasted_iota(I32, (SUBLANES, SUBLANES), 0)
    ci = lax.broadcasted_iota(I32, (SUBLANES, SUBLANES), 1)
    own_ok = (ci <= ri) & (ci < n_tok)
    out_rows = [[] for _ in range(n_tok)]
    for h in range(N_HEADS):
        hs = slice(h * HEAD_DIM, (h + 1) * HEAD_DIM)
        qh = q8[:, hs].astype(BF16)
        lo = jnp.where(own_ok, _nt(qh, k8[:, hs].astype(BF16)), NEG_INF)
        vh = v8[:, hs].astype(BF16)
        for t in range(n_tok):
            ls = jnp.dot(qh, kbuf[slot, h, t].astype(BF16), preferred_element_type=F32)[t:t + 1, :]
            lo_t = lo[t:t + 1, :]
            m = jnp.maximum(jnp.max(ls, axis=1, keepdims=True), jnp.max(lo_t, axis=1, keepdims=True))
            ps = jnp.exp(ls - m)
            po = jnp.exp(lo_t - m)
            den = jnp.sum(ps, axis=1, keepdims=True) + jnp.sum(po, axis=1, keepdims=True)
            ps8 = jnp.broadcast_to(ps, (SUBLANES, ps.shape[1])).astype(BF16)
            po8 = jnp.broadcast_to(po, (SUBLANES, SUBLANES)).astype(BF16)
            o = _nt(ps8, vbuf[slot, h, t].astype(BF16)) + jnp.dot(po8, vh, preferred_element_type=F32)
            out_rows[t].append(o[0:1, :] / den)
    for t in range(n_tok):
        o_ref[pl.ds(t * nb + b, 1), :] = jnp.concatenate(out_rows[t], axis=1).astype(o_ref.dtype)


def _sample_attend(page_flat, sel_flat, q_rows, k_rows, v_rows, ck_t, cv_t, layer, n_batch, n_tok, n_pages):
    kern = functools.partial(_sample_attend_kernel, layer=layer, n_tok=n_tok, n_pages=n_pages)
    full = pl.BlockSpec(q_rows.shape, lambda b, pt, sel: (0, 0))
    sel_cols = MOBA_TOPK * MOBA_BLOCK
    return pl.pallas_call(
        kern,
        grid_spec=pltpu.PrefetchScalarGridSpec(
            num_scalar_prefetch=2,
            grid=(n_batch,),
            in_specs=[full, full, full, pl.BlockSpec(memory_space=pl.ANY), pl.BlockSpec(memory_space=pl.ANY)],
            out_specs=full,
            scratch_shapes=[
                pltpu.VMEM((2, N_HEADS, n_tok, HEAD_DIM, sel_cols), F32),
                pltpu.VMEM((2, N_HEADS, n_tok, HEAD_DIM, sel_cols), F32),
                pltpu.SemaphoreType.DMA((2,)),
            ],
        ),
        out_shape=jax.ShapeDtypeStruct(q_rows.shape, F32),
        compiler_params=_cparams(("arbitrary",), 48),
        name="sample_attend",
    )(page_flat, sel_flat, q_rows, k_rows, v_rows, ck_t, cv_t)


def _mix_kernel(x_ref, up_ref, upp_ref, zc_ref, zcp_ref, gb_ref, gates_ref, ya_ref,
                gt1_ref, sh2_ref, sc2_ref,
                poolw_ref, pscale_ref, wpu_ref, convw_ref, wco_ref, wao_ref, wo_ref, gffn_ref, wrt_ref, br_ref,
                x1_ref, h2_ref, te_ref, tg_ref, zp_ref, zq_ref,
                *, stride, first_ctx_zero, pos_fn):
    tm = up_ref.shape[0]
    cp = upp_ref.shape[0]
    cc = zcp_ref.shape[0]
    up = up_ref[...]
    prev_p = upp_ref[...]
    prev_c = zcp_ref[...]
    if first_ctx_zero:
        first = pl.program_id(1) == 0
        prev_p = jnp.where(first, 0.0, prev_p)
        prev_c = jnp.where(first, 0.0, prev_c)
    zp_ref[0:cp, :] = prev_p
    zp_ref[cp:cp + tm, :] = up
    zc = zc_ref[...]
    zq_ref[0:cc, :] = prev_c
    zq_ref[cc:cc + tm, :] = zc

    pos = pos_fn(tm)
    lane = lax.broadcasted_iota(I32, (1, POOL_W), 1)
    acc = up
    mean = None
    for j in range(1, POOL_CTX + 1):
        acc = acc + zp_ref[cp - j * stride:cp - j * stride + tm, :]
        w = j + 1
        if w in POOL_WINDOWS:
            g = POOL_WINDOWS.index(w)
            cnt = jnp.minimum(pos + 1, w).astype(F32)
            mw = acc / cnt
            mean = mw if mean is None else jnp.where(lane // POOL_GROUP >= g, mw, mean)
    m = (mean - up).astype(BF16)
    pw = poolw_ref[...].reshape(POOL_W, POOL_GROUP).astype(BF16)
    rep = (lax.broadcasted_iota(I32, (POOL_GROUP, POOL_W), 1) % POOL_GROUP
           == lax.broadcasted_iota(I32, (POOL_GROUP, POOL_W), 0)).astype(BF16)
    ri = lax.broadcasted_iota(I32, (POOL_W, POOL_W), 0)
    ci = lax.broadcasted_iota(I32, (POOL_W, POOL_W), 1)
    w_bd = jnp.where(ri // POOL_GROUP == ci // POOL_GROUP, jnp.dot(pw, rep, preferred_element_type=F32), 0.0)
    y = jnp.dot(m, w_bd.astype(BF16), preferred_element_type=F32) * pscale_ref[...]
    y_pool = jnp.dot(y.astype(BF16), wpu_ref[...], preferred_element_type=F32)

    cw = convw_ref[...]
    y_c = (cw[0:1, :] * zq_ref[cc - 2 * stride:cc - 2 * stride + tm, :]
           + cw[1:2, :] * zq_ref[cc - stride:cc - stride + tm, :]
           + cw[2:3, :] * zc)
    y_conv = jnp.dot((gb_ref[...] * y_c).astype(BF16), wco_ref[...], preferred_element_type=F32)

    y_att = jnp.dot(ya_ref[...].astype(BF16), wao_ref[...], preferred_element_type=F32)

    merged = (gates_ref[:, 0:D_MODEL].astype(F32) * y_pool
              + gates_ref[:, D_MODEL:2 * D_MODEL].astype(F32) * y_att
              + gates_ref[:, 2 * D_MODEL:3 * D_MODEL].astype(F32) * y_conv)
    x1 = x_ref[...] + gt1_ref[...] * jnp.dot(merged.astype(BF16), wo_ref[...], preferred_element_type=F32)
    x1_ref[...] = x1
    h2 = _rms(x1, gffn_ref[...]) * (1.0 + sc2_ref[...]) + sh2_ref[...]
    h2_ref[...] = h2

    s = _nt3(wrt_ref[...], h2) + br_ref[...]
    eidx = lax.broadcasted_iota(I32, s.shape, 0)
    vals, idxs = [], []
    for _ in range(TOP_K):
        mx = jnp.max(s, axis=0, keepdims=True)
        ix = jnp.min(jnp.where(s == mx, eidx, N_EXPERTS), axis=0, keepdims=True)
        vals.append(mx)
        idxs.append(ix)
        s = jnp.where(eidx == ix, -jnp.inf, s)
    v = jnp.concatenate(vals, axis=0)
    e = jnp.exp(v - vals[0])
    te_ref[...] = jnp.concatenate(idxs, axis=0)
    tg_ref[...] = e / jnp.sum(e, axis=0, keepdims=True)


def _mix_weight_specs(imap):
    return [
        pl.BlockSpec((len(POOL_WINDOWS), POOL_GROUP, POOL_GROUP), lambda *a: (0, 0, 0)),
        pl.BlockSpec((1, POOL_W), imap),
        pl.BlockSpec((POOL_W, D_MODEL), imap),
        pl.BlockSpec((CONV_K, CONV_W), imap),
        pl.BlockSpec((CONV_W, D_MODEL), imap),
        pl.BlockSpec((ATT_W, D_MODEL), imap),
        pl.BlockSpec((D_MODEL, D_MODEL), imap),
        pl.BlockSpec((1, D_MODEL), imap),
        pl.BlockSpec((N_EXPERTS, D_MODEL), imap),
        pl.BlockSpec((N_EXPERTS, 1), imap),
    ]


def _mix_out_shapes(lead_rows, n_rows_total):
    return (jax.ShapeDtypeStruct(lead_rows + (D_MODEL,), F32),
            jax.ShapeDtypeStruct(lead_rows + (D_MODEL,), F32),
            jax.ShapeDtypeStruct((TOP_K, n_rows_total), I32),
            jax.ShapeDtypeStruct((TOP_K, n_rows_total), F32))


def _mix_prompt(x, up, zc, gb, gates, ya, mod4, layer_b0, weights):
    b, s, _ = x.shape
    tm = TM_MIX
    steps = s // tm
    row = lambda w: pl.BlockSpec((None, tm, w), lambda bi, i: (bi, i, 0))
    modspec = lambda j: pl.BlockSpec((None, None, 1, D_MODEL), lambda bi, i: (bi + layer_b0, j, 0, 0))
    pctx, cctx = 16, 8

    def pos_fn(rows):
        return lax.broadcasted_iota(I32, (rows, 1), 0) + pl.program_id(1) * rows

    kern = functools.partial(_mix_kernel, stride=1, first_ctx_zero=True, pos_fn=pos_fn)
    return pl.pallas_call(
        kern,
        grid=(b, steps),
        in_specs=[
            row(D_MODEL),
            row(POOL_W),
            pl.BlockSpec((None, pctx, POOL_W), lambda bi, i: (bi, jnp.maximum(i * (tm // pctx) - 1, 0), 0)),
            row(CONV_W),
            pl.BlockSpec((None, cctx, CONV_W), lambda bi, i: (bi, jnp.maximum(i * (tm // cctx) - 1, 0), 0)),
            row(CONV_W),
            row(3 * D_MODEL),
            row(ATT_W),
            modspec(2), modspec(3), modspec(4),
        ] + _mix_weight_specs(lambda bi, i: (0, 0)),
        out_specs=(row(D_MODEL), row(D_MODEL),
                   pl.BlockSpec((TOP_K, tm), lambda bi, i: (0, bi * steps + i)),
                   pl.BlockSpec((TOP_K, tm), lambda bi, i: (0, bi * steps + i))),
        out_shape=_mix_out_shapes((b, s), b * s),
        scratch_shapes=[pltpu.VMEM((pctx + tm, POOL_W), F32), pltpu.VMEM((cctx + tm, CONV_W), F32)],
        compiler_params=_cparams(("arbitrary", "arbitrary"), 48),
        name="mix_prompt",
    )(x, up, up, zc, zc, gb, gates, ya, mod4, mod4, mod4, *weights)


def _mix_sample(x, up, pool_ctx, zc, conv_ctx, gb, gates, ya, mod_rows, weights, n_batch, past_len):
    n = x.shape[0]
    full = lambda a: pl.BlockSpec(a.shape, lambda i: (0,) * a.ndim)
    modspec = lambda j: pl.BlockSpec((n, D_MODEL), lambda i: (0, j))

    def pos_fn(rows):
        return lax.broadcasted_iota(I32, (rows, 1), 0) // n_batch + past_len

    kern = functools.partial(_mix_kernel, stride=n_batch, first_ctx_zero=False, pos_fn=pos_fn)
    return pl.pallas_call(
        kern,
        grid=(1,),
        in_specs=[full(x), full(up), full(pool_ctx), full(zc), full(conv_ctx), full(gb), full(gates), full(ya),
                  modspec(2), modspec(3), modspec(4)] + _mix_weight_specs(lambda i: (0, 0)),
        out_specs=(pl.BlockSpec((n, D_MODEL), lambda i: (0, 0)), pl.BlockSpec((n, D_MODEL), lambda i: (0, 0)),
                   pl.BlockSpec((TOP_K, n), lambda i: (0, 0)), pl.BlockSpec((TOP_K, n), lambda i: (0, 0))),
        out_shape=_mix_out_shapes((n,), n),
        scratch_shapes=[pltpu.VMEM((pool_ctx.shape[0] + n, POOL_W), F32),
                        pltpu.VMEM((conv_ctx.shape[0] + n, CONV_W), F32)],
        compiler_params=_cparams(("arbitrary",), 48),
        name="mix_sample",
    )(x, up, pool_ctx, zc, conv_ctx, gb, gates, ya, mod_rows, mod_rows, mod_rows, *weights)


SRC_BITS = 15


def _moe_kernel(te_ref, idx_ref, h_hbm, gate_ref, wup_ref, bup_ref, wdn_ref, bdn_ref, comb_hbm,
                xbuf, ybuf, gsem, ssem):
    i = pl.program_id(0)
    n = pl.num_programs(0)
    slot = i % 2
    tm = xbuf.shape[1]

    def gather(tile, sl):
        def body(r, c):
            tok = idx_ref[tile * tm + r] & ((1 << SRC_BITS) - 1)
            pltpu.make_async_copy(h_hbm.at[pl.ds(tok, 1)], xbuf.at[sl, pl.ds(r, 1)], gsem.at[sl]).start()
            return c
        lax.fori_loop(0, tm, body, 0, unroll=8)

    @pl.when(i == 0)
    def _():
        gather(0, 0)

    @pl.when(i + 1 < n)
    def _():
        gather(i + 1, 1 - slot)

    pltpu.make_async_copy(xbuf.at[slot], xbuf.at[slot], gsem.at[slot]).wait()

    @pl.when(i >= 2)
    def _():
        pltpu.make_async_copy(ybuf.at[slot], ybuf.at[slot], ssem.at[slot]).wait()

    x = xbuf[slot].astype(BF16)
    hu = jnp.dot(x, wup_ref[0], preferred_element_type=F32) + bup_ref[0]
    a = jnp.minimum(hu[:, :D_FF], SWIGLU_LIMIT)
    lin = jnp.clip(hu[:, D_FF:], -SWIGLU_LIMIT, SWIGLU_LIMIT)
    act = (a * jax.nn.sigmoid(SWIGLU_ALPHA * a) * (lin + 1.0)).astype(BF16)
    y = jnp.dot(act, wdn_ref[0], preferred_element_type=F32) + bdn_ref[0]
    ybuf[slot] = y * gate_ref[...]

    def scatter(r, c):
        row = lax.shift_right_logical(idx_ref[i * tm + r], SRC_BITS)
        pltpu.make_async_copy(ybuf.at[slot, pl.ds(r, 1)], comb_hbm.at[pl.ds(row, 1)], ssem.at[slot]).start()
        return c
    lax.fori_loop(0, tm, scatter, 0, unroll=8)

    @pl.when(i == n - 1)
    def _():
        pltpu.make_async_copy(ybuf.at[slot], ybuf.at[slot], ssem.at[slot]).wait()

        @pl.when(n >= 2)
        def _():
            pltpu.make_async_copy(ybuf.at[1 - slot], ybuf.at[1 - slot], ssem.at[1 - slot]).wait()


def _moe(tile_e, row_src, row_dst, h_all, row_gate, w_up, b_up, w_down, b_down, n_comb_rows):
    n_tiles = tile_e.shape[0]
    tm = TM_MOE
    assert h_all.shape[0] <= (1 << SRC_BITS) and n_comb_rows <= (1 << (32 - SRC_BITS))
    row_idx = row_src | lax.shift_left(row_dst, SRC_BITS)
    return pl.pallas_call(
        _moe_kernel,
        grid_spec=pltpu.PrefetchScalarGridSpec(
            num_scalar_prefetch=2,
            grid=(n_tiles,),
            in_specs=[
                pl.BlockSpec(memory_space=pl.ANY),
                pl.BlockSpec((tm, 1), lambda i, te, ix: (i, 0)),
                pl.BlockSpec((1, D_MODEL, 2 * D_FF), lambda i, te, ix: (te[i], 0, 0)),
                pl.BlockSpec((1, 1, 2 * D_FF), lambda i, te, ix: (te[i], 0, 0)),
                pl.BlockSpec((1, D_FF, D_MODEL), lambda i, te, ix: (te[i], 0, 0)),
                pl.BlockSpec((1, 1, D_MODEL), lambda i, te, ix: (te[i], 0, 0)),
            ],
            out_specs=pl.BlockSpec(memory_space=pl.ANY),
            scratch_shapes=[
                pltpu.VMEM((2, tm, D_MODEL), F32),
                pltpu.VMEM((2, tm, D_MODEL), F32),
                pltpu.SemaphoreType.DMA((2,)),
                pltpu.SemaphoreType.DMA((2,)),
            ],
        ),
        out_shape=jax.ShapeDtypeStruct((n_comb_rows, D_MODEL), F32),
        compiler_params=_cparams(("arbitrary",), 48),
        name="moe_experts",
    )(tile_e, row_idx, h_all, row_gate, w_up, b_up, w_down, b_down)


def _route(top_e, top_g, n_slot_rows):
    n_tok = top_e.shape[1]
    nk = TOP_K * n_tok
    tm = TM_MOE
    n_tiles = -(-(nk + N_EXPERTS * (tm - 1)) // tm)
    n_rows = n_tiles * tm
    e_flat = top_e.reshape(nk)
    g_flat = top_g.reshape(nk)
    order = jnp.argsort(e_flat).astype(I32)
    e_sorted = e_flat[order]
    counts = jnp.sum((e_flat[:, None] == jnp.arange(N_EXPERTS, dtype=I32)[None, :]).astype(I32), axis=0)
    padded = (counts + tm - 1) // tm * tm
    start = jnp.cumsum(counts) - counts
    pad_end = jnp.cumsum(padded)
    pad_start = pad_end - padded
    dest = pad_start[e_sorted] + jnp.arange(nk, dtype=I32) - start[e_sorted]
    kk = order // n_tok
    tok = order - kk * n_tok
    row_src = jnp.zeros((n_rows,), I32).at[dest].set(tok)
    row_dst = jnp.full((n_rows,), -1, I32).at[dest].set(kk * n_slot_rows + tok)
    row_gate = jnp.zeros((n_rows,), F32).at[dest].set(g_flat[order])
    is_pad = row_dst < 0
    pad_rank = jnp.cumsum(is_pad.astype(I32)) - 1
    gap = n_slot_rows - n_tok
    n_gap = TOP_K * gap
    assert n_rows - nk >= n_gap
    gap_k = pad_rank // max(gap, 1)
    pad_dst = jnp.where(pad_rank < n_gap, gap_k * n_slot_rows + n_tok + (pad_rank - gap_k * gap),
                        TOP_K * n_slot_rows + pad_rank - n_gap)
    row_dst = jnp.where(is_pad, pad_dst, row_dst)
    tile_e = jnp.minimum(jnp.searchsorted(pad_end, jnp.arange(n_tiles, dtype=I32) * tm, side='right'),
                         N_EXPERTS - 1).astype(I32)
    n_comb_rows = TOP_K * n_slot_rows + (n_rows - nk) - n_gap
    return tile_e, row_src, row_dst, row_gate.reshape(n_rows, 1), n_comb_rows


def _combine_kernel(x1_ref, c0_ref, c1_ref, c2_ref, c3_ref, gt2_ref, gfin_ref, o_ref, *, final):
    ffn = (c0_ref[...] + c1_ref[...]) + (c2_ref[...] + c3_ref[...])
    x2 = x1_ref[...] + gt2_ref[...] * ffn
    o_ref[...] = _rms(x2, gfin_ref[...]) if final else x2


def _combine_prompt(x1, comb, mod4, layer_b0, g_final, n_slot_rows, final):
    b, s, _ = x1.shape
    tm = TM_COMB
    steps = s // tm
    cspec = lambda k: pl.BlockSpec((tm, D_MODEL), lambda bi, i: (k * (n_slot_rows // tm) + bi * steps + i, 0))
    return pl.pallas_call(
        functools.partial(_combine_kernel, final=final),
        grid=(b, steps),
        in_specs=[pl.BlockSpec((None, tm, D_MODEL), lambda bi, i: (bi, i, 0)),
                  cspec(0), cspec(1), cspec(2), cspec(3),
                  pl.BlockSpec((None, None, 1, D_MODEL), lambda bi, i: (bi + layer_b0, 5, 0, 0)),
                  pl.BlockSpec((1, D_MODEL), lambda bi, i: (0, 0))],
        out_specs=pl.BlockSpec((None, tm, D_MODEL), lambda bi, i: (bi, i, 0)),
        out_shape=jax.ShapeDtypeStruct(x1.shape, F32),
        compiler_params=_cparams(("arbitrary", "arbitrary"), 48),
        name="combine_prompt",
    )(x1, comb, comb, comb, comb, mod4, g_final)


def _combine_sample(x1, comb, mod_rows, g_final, n_slot_rows, row0, final):
    n = x1.shape[0]
    cspec = lambda k: pl.BlockSpec((n, D_MODEL), lambda i: ((k * n_slot_rows + row0) // n, 0))
    return pl.pallas_call(
        functools.partial(_combine_kernel, final=final),
        grid=(1,),
        in_specs=[pl.BlockSpec((n, D_MODEL), lambda i: (0, 0)),
                  cspec(0), cspec(1), cspec(2), cspec(3),
                  pl.BlockSpec((n, D_MODEL), lambda i: (0, 5)),
                  pl.BlockSpec((1, D_MODEL), lambda i: (0, 0))],
        out_specs=pl.BlockSpec((n, D_MODEL), lambda i: (0, 0)),
        out_shape=jax.ShapeDtypeStruct(x1.shape, F32),
        compiler_params=_cparams(("arbitrary",), 32),
        name="combine_sample",
    )(x1, comb, comb, comb, comb, mod_rows, g_final)


def kernel(x_prompt, x_sample, cache_k, cache_v, page_table, state_pool, state_conv, c_prompt, c_sample, w_ada, b_ada, g_mix, g_ffn, w_in, pool_w, pool_scale, w_pool_up, w_att_o, conv_w, w_conv_out, w_out, w_router, b_router, w_up, b_up, w_down, b_down, g_final):
    b, s, _ = x_prompt.shape
    db, t_new, _ = x_sample.shape
    n_pages = page_table.shape[1]
    past_len = n_pages * PAGE_SIZE
    assert n_pages % (MOBA_BLOCK // PAGE_SIZE) == 0 and n_pages % PAGES_PER_CHUNK == 0
    assert (n_pages // PAGES_PER_CHUNK) % 2 == 0 and n_pages // (MOBA_BLOCK // PAGE_SIZE) >= MOBA_TOPK
    n_p = b * s
    n_s = db * t_new
    n_tok = n_p + n_s
    n_slot_rows = -(-n_tok // TM_COMB) * TM_COMB

    half = HEAD_DIM // 2
    inv = ROPE_THETA ** (-jnp.arange(half, dtype=F32) / half)
    inv_lanes = jnp.tile(inv, LANES // half).reshape(1, LANES)
    cos_p, sin_p = _rope_tables(inv_lanes, s, 0)
    cos_s8, sin_s8 = _rope_tables(inv_lanes, SUBLANES, past_len)
    cos_s = jnp.repeat(cos_s8[:t_new], db, axis=0)
    sin_s = jnp.repeat(sin_s8[:t_new], db, axis=0)

    mod = _ada_mod(jnp.concatenate([c_prompt, c_sample], axis=0), w_ada, b_ada)

    ck_t = jnp.transpose(cache_k, (0, 1, 3, 4, 2))
    cv_t = jnp.transpose(cache_v, (0, 1, 3, 4, 2))
    page_flat = page_table.reshape(-1).astype(I32)

    hp = x_prompt
    hs = jnp.transpose(x_sample, (1, 0, 2)).reshape(n_s, D_MODEL)
    g_final2 = g_final.reshape(1, D_MODEL)
    outs = [[] for _ in range(8)]
    for l in range(DEPTH):
        final = l == DEPTH - 1
        mod4 = mod[l].reshape(b + db, 6, 1, D_MODEL)
        mod_rows = jnp.tile(mod[l, b:], (t_new, 1))
        w_in_b = w_in[l].astype(BF16)
        g_mix2 = g_mix[l].reshape(1, D_MODEL)
        mix_w = (pool_w[l], pool_scale[l].reshape(1, POOL_W), w_pool_up[l].astype(BF16), conv_w[l],
                 w_conv_out[l].astype(BF16), w_att_o[l].astype(BF16), w_out[l].astype(BF16),
                 g_ffn[l].reshape(1, D_MODEL), jnp.transpose(w_router[l]), b_router[l].reshape(N_EXPERTS, 1))

        up_p, q_p, k_p, v_p, gb_p, zc_p, gates_p = _proj_prompt(hp, mod4, 0, g_mix2, cos_p, sin_p, w_in_b)
        ya_p = _attn_prompt(q_p, k_p, v_p)
        x1_p, h2_p, te_p, tg_p = _mix_prompt(hp, up_p, zc_p, gb_p, gates_p, ya_p, mod4, 0, mix_w)

        up_s, q_s, k_s, v_s, gb_s, zc_s, gates_s = _proj_sample(hs, mod_rows, g_mix2, cos_s, sin_s, w_in_b)
        sel = _sample_select(page_flat, q_s, ck_t, l, db, t_new, n_pages)
        sel_flat = sel.reshape(db, N_HEADS, SUBLANES, LANES)[:, :, :t_new, :MOBA_TOPK].reshape(-1)
        ya_s = _sample_attend(page_flat, sel_flat, q_s, k_s, v_s, ck_t, cv_t, l, db, t_new, n_pages)
        pool_ctx = jnp.transpose(state_pool[l], (1, 0, 2)).reshape(POOL_CTX * db, POOL_W)
        conv_ctx = jnp.transpose(state_conv[l], (1, 0, 2)).reshape((CONV_K - 1) * db, CONV_W)
        x1_s, h2_s, te_s, tg_s = _mix_sample(hs, up_s, pool_ctx, zc_s, conv_ctx, gb_s, gates_s, ya_s,
                                             mod_rows, mix_w, db, past_len)

        h_all = jnp.concatenate([h2_p.reshape(n_p, D_MODEL), h2_s], axis=0)
        top_e = jnp.concatenate([te_p, te_s], axis=1)
        top_g = jnp.concatenate([tg_p, tg_s], axis=1)
        tile_e, row_src, row_dst, row_gate, n_comb_rows = _route(top_e, top_g, n_slot_rows)
        comb = _moe(tile_e, row_src, row_dst, h_all, row_gate, w_up[l].astype(BF16),
                    b_up[l].reshape(N_EXPERTS, 1, 2 * D_FF), w_down[l].astype(BF16),
                    b_down[l].reshape(N_EXPERTS, 1, D_MODEL), n_comb_rows)
        hp = _combine_prompt(x1_p, comb, mod4, 0, g_final2, n_slot_rows, final)
        hs = _combine_sample(x1_s, comb, mod_rows, g_final2, n_slot_rows, n_p, final)

        def to_bt(a):
            return jnp.transpose(a.reshape(t_new, db, -1), (1, 0, 2))

        outs[0].append(k_p.reshape(b, s // PAGE_SIZE, PAGE_SIZE, N_HEADS, HEAD_DIM))
        outs[1].append(v_p.reshape(b, s // PAGE_SIZE, PAGE_SIZE, N_HEADS, HEAD_DIM))
        outs[2].append(to_bt(k_s).reshape(db, t_new, N_HEADS, HEAD_DIM))
        outs[3].append(to_bt(v_s).reshape(db, t_new, N_HEADS, HEAD_DIM))
        outs[4].append(up_p[:, s - POOL_CTX:, :])
        outs[5].append(jnp.concatenate([state_pool[l], to_bt(up_s)], axis=1)[:, -POOL_CTX:])
        outs[6].append(zc_p[:, s - (CONV_K - 1):, :])
        outs[7].append(jnp.concatenate([state_conv[l], to_bt(zc_s)], axis=1)[:, -(CONV_K - 1):])

    y_prompt = hp
    y_sample = jnp.transpose(hs.reshape(t_new, db, D_MODEL), (1, 0, 2))
    return (y_prompt, y_sample) + tuple(jnp.stack(o) for o in outs)
```
